```python
import math
import jax, jax.numpy as jnp
from jax import lax
import numpy as np

D_MODEL = 1024
BATCH = 4
SEQ = 4096
DEPTH = 2

CHUNK = 64
Q_BLOCK = 128
N_A_LAYERS = DEPTH // 2
N_B_LAYERS = DEPTH - N_A_LAYERS
HEAD_DIM = 64
FOX_HEADS = D_MODEL // HEAD_DIM
DIFF_HEADS = D_MODEL // (2 * HEAD_DIM)
DIFF_V_DIM = 2 * HEAD_DIM
D_FF = 2816
CONV_WIDTH = 3
ROPE_THETA = 10000.0
LN_EPS = 1e-5
RMS_EPS = 1e-5
DN_ALPHA = (2 * DEPTH) ** 0.25
DN_BETA = (8 * DEPTH) ** -0.25

kernel_name = 'yoco_fox_diffattn_convffn_deepnorm'


def _layer_norm(x, g, b):
    xf = x.astype(jnp.float32)
    mu = jnp.mean(xf, axis=-1, keepdims=True)
    var = jnp.mean(jnp.square(xf - mu), axis=-1, keepdims=True)
    y = (xf - mu) * lax.rsqrt(var + LN_EPS)
    return (y * g.astype(jnp.float32) + b.astype(jnp.float32)).astype(x.dtype)


def _rope_tables(seq_len):
    inv_freq = 1.0 / (ROPE_THETA ** (jnp.arange(0, HEAD_DIM, 2, dtype=jnp.float32) / HEAD_DIM))
    ang = jnp.arange(seq_len, dtype=jnp.float32)[:, None] * inv_freq[None, :]
    return jnp.cos(ang), jnp.sin(ang)


def _apply_rope(t, cos, sin):
    half = t.shape[-1] // 2
    t1, t2 = t[..., :half], t[..., half:]
    c = cos.astype(t.dtype)
    s = sin.astype(t.dtype)
    return jnp.concatenate([t1 * c - t2 * s, t2 * c + t1 * s], axis=-1)


def _heads(t, n_heads, d):
    b, s, _ = t.shape
    return t.reshape(b, s, n_heads, d).transpose(0, 2, 1, 3)


def _merge_blocks(o):
    nb, b, h, qb, dv = o.shape
    return o.transpose(1, 0, 3, 2, 4).reshape(b, nb * qb, h * dv)


def _fox_attention(x, w_in, b_f, w_out):
    b, s, d = x.shape
    proj = x @ w_in
    q = _heads(proj[..., :d], FOX_HEADS, HEAD_DIM)
    k = _heads(proj[..., d:2 * d], FOX_HEADS, HEAD_DIM)
    v = _heads(proj[..., 2 * d:3 * d], FOX_HEADS, HEAD_DIM)
    f_logit = proj[..., 3 * d:].astype(jnp.float32) + b_f.astype(jnp.float32)
    c = jnp.cumsum(jax.nn.log_sigmoid(f_logit), axis=1).transpose(0, 2, 1)
    scale = HEAD_DIM ** -0.5
    kpos = jnp.arange(s)

    def block(i):
        start = i * Q_BLOCK
        qb = lax.dynamic_slice_in_dim(q, start, Q_BLOCK, axis=2)
        cb = lax.dynamic_slice_in_dim(c, start, Q_BLOCK, axis=2)
        logits = jnp.einsum('bhqd,bhkd->bhqk', qb, k).astype(jnp.float32) * scale
        logits = logits + cb[..., :, None] - c[..., None, :]
        qpos = start + jnp.arange(Q_BLOCK)
        mask = kpos[None, :] <= qpos[:, None]
        p = jax.nn.softmax(jnp.where(mask, logits, -jnp.inf), axis=-1).astype(v.dtype)
        return jnp.einsum('bhqk,bhkd->bhqd', p, v)

    o = lax.map(block, jnp.arange(s // Q_BLOCK))
    return _merge_blocks(o) @ w_out


def _shared_kv(x, w_kv, cos, sin):
    b, s, _ = x.shape
    kw = 2 * DIFF_HEADS * HEAD_DIM
    proj = x @ w_kv
    k = proj[..., :kw].reshape(b, s, 2, DIFF_HEADS, HEAD_DIM).transpose(2, 0, 3, 1, 4)
    k1 = _apply_rope(k[0], cos, sin)
    k2 = _apply_rope(k[1], cos, sin)
    v = _heads(proj[..., kw:], DIFF_HEADS, DIFF_V_DIM)
    return k1, k2, v


def _diff_attention(x, k1, k2, v, w_q, lam_p, subln_g, w_out, lambda_init, cos, sin):
    b, s, _ = x.shape
    qp = (x @ w_q).reshape(b, s, 2, DIFF_HEADS, HEAD_DIM).transpose(2, 0, 3, 1, 4)
    q1 = _apply_rope(qp[0], cos, sin)
    q2 = _apply_rope(qp[1], cos, sin)
    lp = lam_p.astype(jnp.float32)
    lam = jnp.exp(jnp.sum(lp[0] * lp[1])) - jnp.exp(jnp.sum(lp[2] * lp[3])) + lambda_init
    scale = HEAD_DIM ** -0.5
    kpos = jnp.arange(s)

    def block(i):
        start = i * Q_BLOCK
        qb1 = lax.dynamic_slice_in_dim(q1, start, Q_BLOCK, axis=2)
        qb2 = lax.dynamic_slice_in_dim(q2, start, Q_BLOCK, axis=2)
        qpos = start + jnp.arange(Q_BLOCK)
        mask = kpos[None, :] < ((qpos // CHUNK) + 1)[:, None] * CHUNK
        s1 = jnp.einsum('bhqd,bhkd->bhqk', qb1, k1).astype(jnp.float32) * scale
        s2 = jnp.einsum('bhqd,bhkd->bhqk', qb2, k2).astype(jnp.float32) * scale
        p1 = jax.nn.softmax(jnp.where(mask, s1, -jnp.inf), axis=-1)
        p2 = jax.nn.softmax(jnp.where(mask, s2, -jnp.inf), axis=-1)
        a = (p1 - lam * p2).astype(v.dtype)
        return jnp.einsum('bhqk,bhkd->bhqd', a, v)

    o = lax.map(block, jnp.arange(s // Q_BLOCK)).astype(jnp.float32)
    o = o * lax.rsqrt(jnp.mean(jnp.square(o), axis=-1, keepdims=True) + RMS_EPS)
    o = (o * subln_g.astype(jnp.float32) * (1.0 - lambda_init)).astype(x.dtype)
    return _merge_blocks(o) @ w_out


def _conv_ffn(x, w_in, conv_w, conv_b, w_out):
    s = x.shape[1]
    u = x @ w_in
    up = jnp.pad(u, ((0, 0), (CONV_WIDTH - 1, 0), (0, 0)))
    u = sum(conv_w[j] * up[:, j:j + s] for j in range(CONV_WIDTH)) + conv_b
    g, val = u[..., :D_FF], u[..., D_FF:]
    return (jax.nn.silu(g) * val) @ w_out


def setup_inputs(seed: int = 0) -> dict:
    key = jax.random.key(seed)
    ks = jax.random.split(key, 20)
    d = D_MODEL
    f32 = jnp.float32
    nrm = lambda k, shape, sc: jax.random.normal(k, shape, f32) * sc
    return {
        'x': jax.random.normal(ks[0], (BATCH, SEQ, d), f32),
        'a_w_in': nrm(ks[1], (N_A_LAYERS, d, 3 * d + FOX_HEADS), d ** -0.5),
        'a_b_f': jax.random.uniform(ks[2], (N_A_LAYERS, FOX_HEADS), f32, 1.0, 6.0),
        'a_w_out': nrm(ks[3], (N_A_LAYERS, d, d), DN_BETA * d ** -0.5),
        'kv_w': nrm(ks[4], (d, 2 * DIFF_HEADS * HEAD_DIM + DIFF_HEADS * DIFF_V_DIM), d ** -0.5),
        'b_w_q': nrm(ks[5], (N_B_LAYERS, d, 2 * DIFF_HEADS * HEAD_DIM), d ** -0.5),
        'b_lambda': nrm(ks[6], (N_B_LAYERS, 4, HEAD_DIM), 0.1),
        'b_subln_g': 1.0 + nrm(ks[7], (N_B_LAYERS, DIFF_V_DIM), 0.02),
        'b_w_out': nrm(ks[8], (N_B_LAYERS, DIFF_HEADS * DIFF_V_DIM, d), DN_BETA * d ** -0.5),
        'ffn_w_in': nrm(ks[9], (DEPTH, d, 2 * D_FF), d ** -0.5),
        'ffn_conv_w': nrm(ks[10], (DEPTH, CONV_WIDTH, 2 * D_FF), CONV_WIDTH ** -0.5),
        'ffn_conv_b': nrm(ks[11], (DEPTH, 2 * D_FF), 0.02),
        'ffn_w_out': nrm(ks[12], (DEPTH, D_FF, d), DN_BETA * D_FF ** -0.5),
        'ln_attn_g': 1.0 + nrm(ks[13], (DEPTH, d), 0.02),
        'ln_attn_b': nrm(ks[14], (DEPTH, d), 0.02),
        'ln_ffn_g': 1.0 + nrm(ks[15], (DEPTH, d), 0.02),
        'ln_ffn_b': nrm(ks[16], (DEPTH, d), 0.02),
    }


def reference(x, a_w_in, a_b_f, a_w_out, kv_w, b_w_q, b_lambda, b_subln_g, b_w_out,
              ffn_w_in, ffn_conv_w, ffn_conv_b, ffn_w_out,
              ln_attn_g, ln_attn_b, ln_ffn_g, ln_ffn_b):
    cos, sin = _rope_tables(x.shape[1])
    k1 = k2 = v = None
    for layer in range(DEPTH):
        if layer < N_A_LAYERS:
            h = _fox_attention(x, a_w_in[layer], a_b_f[layer], a_w_out[layer])
        else:
            j = layer - N_A_LAYERS
            lambda_init = 0.8 - 0.6 * math.exp(-0.3 * layer)
            h = _diff_attention(x, k1, k2, v, b_w_q[j], b_lambda[j], b_subln_g[j], b_w_out[j],
                                lambda_init, cos, sin)
        x = _layer_norm(DN_ALPHA * x + h, ln_attn_g[layer], ln_attn_b[layer])
        f = _conv_ffn(x, ffn_w_in[layer], ffn_conv_w[layer], ffn_conv_b[layer], ffn_w_out[layer])
        x = _layer_norm(DN_ALPHA * x + f, ln_ffn_g[layer], ln_ffn_b[layer])
        if layer == N_A_LAYERS - 1:
            k1, k2, v = _shared_kv(x, kv_w, cos, sin)
    return x
```

```python
import functools
import math

import jax
import jax.numpy as jnp
from jax import lax
from jax.experimental import pallas as pl
from jax.experimental.pallas import tpu as pltpu

D_MODEL = 1024
HEAD_DIM = 64
FOX_HEADS = D_MODEL // HEAD_DIM
DIFF_HEADS = D_MODEL // (2 * HEAD_DIM)
DIFF_V_DIM = 2 * HEAD_DIM
D_FF = 2816
CHUNK = 64
DEPTH = 2
ROPE_THETA = 10000.0
LN_EPS = 1e-5
RMS_EPS = 1e-5
DN_ALPHA = (2 * DEPTH) ** 0.25
QK_SCALE = HEAD_DIM ** -0.5

LANES = 128
SUBLANES = 8
VMEM_LIMIT_BYTES = 56 * 1024 * 1024

ROW_TILE = 512
ATTN_TQ = 512
ATTN_TK = 512
FF_TILE = 256

F32 = jnp.float32
BF16 = jnp.bfloat16
NT_DIMS = (((1,), (1,)), ((), ()))


def _resident(shape):
    zeros = (0,) * len(shape)
    return pl.BlockSpec(shape, lambda *_: zeros, pipeline_mode=pl.Buffered(1))


def _layer_norm(y, g, b):
    mu = jnp.mean(y, axis=-1, keepdims=True)
    yc = y - mu
    var = jnp.mean(yc * yc, axis=-1, keepdims=True)
    return yc * lax.rsqrt(var + LN_EPS) * g + b


def _fox_proj_kernel(x_ref, w_ref, wf_ref, bf_ref, q_ref, k_ref, v_ref, c_ref, carry_ref,
                     *, tiles_per_seq):
    d = D_MODEL
    xb = x_ref[...].astype(BF16)
    q = jnp.dot(xb, w_ref[:, 0:d], preferred_element_type=F32)
    q_ref[...] = (q * QK_SCALE).astype(BF16)
    k_ref[...] = jnp.dot(xb, w_ref[:, d:2 * d], preferred_element_type=F32).astype(BF16)
    v_ref[...] = jnp.dot(xb, w_ref[:, 2 * d:3 * d], preferred_element_type=F32).astype(BF16)

    fl = jnp.dot(xb, wf_ref[...], preferred_element_type=F32) + bf_ref[...]
    g = jnp.minimum(fl, 0.0) - jnp.log1p(jnp.exp(-jnp.abs(fl)))

    tm = g.shape[0]
    row = lax.broadcasted_iota(jnp.int32, (tm, tm), 0)
    col = lax.broadcasted_iota(jnp.int32, (tm, tm), 1)
    tri = (col <= row).astype(BF16)
    g1 = g.astype(BF16)
    r1 = g - g1.astype(F32)
    g2 = r1.astype(BF16)
    g3 = (r1 - g2.astype(F32)).astype(BF16)
    cs = (jnp.dot(tri, g1, preferred_element_type=F32)
          + jnp.dot(tri, g2, preferred_element_type=F32)
          + jnp.dot(tri, g3, preferred_element_type=F32))

    @pl.when(pl.program_id(0) % tiles_per_seq == 0)
    def _():
        carry_ref[...] = jnp.zeros_like(carry_ref)

    c = cs + carry_ref[...]
    c_ref[...] = c
    carry_ref[...] = c[tm - 1:tm, :]


def _fox_proj(x2, w_qkv, w_f, b_f, seq_len):
    n, d = x2.shape
    tm = ROW_TILE
    row_spec = pl.BlockSpec((tm, d), lambda i: (i, 0))
    return pl.pallas_call(
        functools.partial(_fox_proj_kernel, tiles_per_seq=seq_len // tm),
        grid=(n // tm,),
        in_specs=[row_spec, _resident(w_qkv.shape), _resident(w_f.shape), _resident(b_f.shape)],
        out_specs=[row_spec, row_spec, row_spec, pl.BlockSpec((tm, FOX_HEADS), lambda i: (i, 0))],
        out_shape=[jax.ShapeDtypeStruct((n, d), BF16)] * 3
        + [jax.ShapeDtypeStruct((n, FOX_HEADS), F32)],
        scratch_shapes=[pltpu.VMEM((1, FOX_HEADS), F32)],
        compiler_params=pltpu.CompilerParams(
            dimension_semantics=("arbitrary",), vmem_limit_bytes=VMEM_LIMIT_BYTES),
        name="fox_proj",
    )(x2, w_qkv, w_f, b_f)


def _fox_attn_kernel(q_ref, k_ref, v_ref, ccol_ref, crow_ref, o_ref, *, tq, tk):
    pair = pl.program_id(1)
    seq_len = q_ref.shape[1]
    lane = lax.broadcasted_iota(jnp.int32, (1, LANES), 1)
    lo = lane < HEAD_DIM
    head_lane = lax.broadcasted_iota(jnp.int32, (1, FOX_HEADS), 1)
    col_minus_row = (lax.broadcasted_iota(jnp.int32, (tq, tk), 1)
                     - lax.broadcasted_iota(jnp.int32, (tq, tk), 0))
    zero_bf = jnp.zeros((), BF16)

    def q_block(qi, _):
        q_start = pl.multiple_of(qi * tq, tq)
        q = q_ref[0, pl.ds(q_start, tq), :]
        q_heads = (jnp.where(lo, q, zero_bf), jnp.where(lo, zero_bf, q))
        cc = ccol_ref[0, pl.ds(q_start, tq), :]
        cb = [jnp.sum(jnp.where(head_lane == 2 * pair + h, cc, 0.0), axis=1, keepdims=True)
              for h in range(2)]

        def kv_block(kj, carry, masked):
            k_start = pl.multiple_of(kj * tk, tk)
            k = k_ref[0, pl.ds(k_start, tk), :]
            v = v_ref[0, pl.ds(k_start, tk), :]
            v_heads = (jnp.where(lo, v, zero_bf), jnp.where(lo, zero_bf, v))
            ck = crow_ref[0, 0, kj]
            stats, acc = carry
            new_stats, alphas, pv = [], [], None
            for h in range(2):
                m_prev, l_prev = stats[h]
                s = lax.dot_general(q_heads[h], k, NT_DIMS, preferred_element_type=F32)
                z = s - ck[h:h + 1, :]
                if masked:
                    z = jnp.where(col_minus_row <= q_start - k_start, z, -jnp.inf)
                m_new = jnp.maximum(m_prev, jnp.max(z, axis=1, keepdims=True) + cb[h])
                alpha = jnp.exp(m_prev - m_new)
                p = jnp.exp(z - (m_new - cb[h]))
                l_new = alpha * l_prev + jnp.sum(p, axis=1, keepdims=True)
                part = jnp.dot(p.astype(BF16), v_heads[h], preferred_element_type=F32)
                pv = part if pv is None else pv + part
                new_stats.append((m_new, l_new))
                alphas.append(alpha)
            acc = acc * jnp.where(lo, alphas[0], alphas[1]) + pv
            return tuple(new_stats), acc

        init_stat = (jnp.full((tq, 1), -jnp.inf, F32), jnp.zeros((tq, 1), F32))
        carry = ((init_stat, init_stat), jnp.zeros((tq, LANES), F32))
        carry = lax.fori_loop(0, qi * (tq // tk), lambda j, c: kv_block(j, c, False), carry)
        for dj in range(tq // tk):
            carry = kv_block(qi * (tq // tk) + dj, carry, True)
        stats, acc = carry
        inv = jnp.where(lo, 1.0 / stats[0][1], 1.0 / stats[1][1])
        o_ref[0, pl.ds(q_start, tq), :] = (acc * inv).astype(BF16)
        return 0

    lax.fori_loop(0, seq_len // tq, q_block, 0)


def _fox_attn(q, k, v, ccol, crow):
    b, s, d = q.shape
    pairs = d // LANES
    tq, tk = ATTN_TQ, ATTN_TK
    seq_spec = pl.BlockSpec((1, s, LANES), lambda bi, pi: (bi, 0, pi))
    return pl.pallas_call(
        functools.partial(_fox_attn_kernel, tq=tq, tk=tk),
        grid=(b, pairs),
        in_specs=[seq_spec, seq_spec, seq_spec,
                  pl.BlockSpec((1, s, FOX_HEADS), lambda bi, pi: (bi, 0, 0)),
                  pl.BlockSpec((1, 1, s // tk, 2, tk), lambda bi, pi: (bi, pi, 0, 0, 0))],
        out_specs=seq_spec,
        out_shape=jax.ShapeDtypeStruct((b, s, d), BF16),
        compiler_params=pltpu.CompilerParams(
            dimension_semantics=("parallel", "parallel"), vmem_limit_bytes=VMEM_LIMIT_BYTES),
        name="fox_attn",
    )(q, k, v, ccol, crow)


def _outproj_ln_kernel(o_ref, w_ref, x_ref, g_ref, b_ref, y_ref):
    h = jnp.dot(o_ref[...], w_ref[...], preferred_element_type=F32)
    y_ref[...] = _layer_norm(DN_ALPHA * x_ref[...] + h, g_ref[...], b_ref[...])


def _outproj_ln(o2, w, x2, g, b):
    n, d = x2.shape
    tm = ROW_TILE
    row_spec = pl.BlockSpec((tm, d), lambda i: (i, 0))
    return pl.pallas_call(
        _outproj_ln_kernel,
        grid=(n // tm,),
        in_specs=[row_spec, _resident(w.shape), row_spec, _resident(g.shape), _resident(b.shape)],
        out_specs=row_spec,
        out_shape=jax.ShapeDtypeStruct((n, d), F32),
        compiler_params=pltpu.CompilerParams(
            dimension_semantics=("parallel",), vmem_limit_bytes=VMEM_LIMIT_BYTES),
        name="outproj_ln",
    )(o2, w, x2, g, b)


def _ffn_kernel(x_ref, win_ref, cw_ref, cb_ref, wout_ref, g_ref, b_ref, y_ref,
                carry_ref, acc_ref, *, tiles_per_seq, n_chunks):
    @pl.when(pl.program_id(0) % tiles_per_seq == 0)
    def _():
        carry_ref[...] = jnp.zeros_like(carry_ref)

    x = x_ref[...]
    xb = x.astype(BF16)
    tm = x.shape[0]
    tf = win_ref.shape[2]
    row8 = lax.broadcasted_iota(jnp.int32, (SUBLANES, tf), 0)
    acc_ref[...] = jnp.zeros_like(acc_ref)

    def conv_half(idx):
        u = jnp.dot(xb, win_ref[idx], preferred_element_type=F32)
        prev = carry_ref[idx]
        carry_ref[idx] = u[tm - SUBLANES:tm, :]
        w = cw_ref[idx]
        out = w[2:3, :] * u + cb_ref[idx]
        for shift in (1, 2):
            rolled = pltpu.roll(u, shift, 0)
            top = jnp.where(row8 < shift, pltpu.roll(prev, shift, 0), rolled[0:SUBLANES, :])
            shifted = jnp.concatenate([top, rolled[SUBLANES:, :]], axis=0)
            out = out + w[2 - shift:3 - shift, :] * shifted
        return out

    def chunk(c, _):
        yg = conv_half(c)
        yv = conv_half(n_chunks + c)
        h = yg * (1.0 / (1.0 + jnp.exp(-yg))) * yv
        acc_ref[...] += jnp.dot(h.astype(BF16), wout_ref[c], preferred_element_type=F32)
        return 0

    lax.fori_loop(0, n_chunks, chunk, 0)
    y_ref[...] = _layer_norm(DN_ALPHA * x + acc_ref[...], g_ref[...], b_ref[...])


def _ffn(x2, w_in, conv_w, conv_b, w_out, g, b, seq_len):
    n, d = x2.shape
    tm, tf = ROW_TILE, FF_TILE
    n_chunks = D_FF // tf
    win3 = w_in.astype(BF16).reshape(d, 2 * n_chunks, tf).transpose(1, 0, 2)
    cw3 = conv_w.reshape(3, 2 * n_chunks, tf).transpose(1, 0, 2)
    cb3 = conv_b.reshape(2 * n_chunks, 1, tf)
    wout3 = w_out.astype(BF16).reshape(n_chunks, tf, d)
    row_spec = pl.BlockSpec((tm, d), lambda i: (i, 0))
    return pl.pallas_call(
        functools.partial(_ffn_kernel, tiles_per_seq=seq_len // tm, n_chunks=n_chunks),
        grid=(n // tm,),
        in_specs=[row_spec, _resident(win3.shape), _resident(cw3.shape), _resident(cb3.shape),
                  _resident(wout3.shape), _resident(g.shape), _resident(b.shape)],
        out_specs=row_spec,
        out_shape=jax.ShapeDtypeStruct((n, d), F32),
        scratch_shapes=[pltpu.VMEM((2 * n_chunks, SUBLANES, tf), F32),
                        pltpu.VMEM((tm, d), F32)],
        compiler_params=pltpu.CompilerParams(
            dimension_semantics=("arbitrary",), vmem_limit_bytes=VMEM_LIMIT_BYTES),
        name="conv_ffn",
    )(x2, win3, cw3, cb3, wout3, g, b)


def _diff_proj_kernel(x_ref, w_ref, cos_ref, sin_ref, q_ref, k_ref, v_ref):
    d = D_MODEL
    xb = x_ref[...].astype(BF16)
    cosf = cos_ref[...]
    sinf = sin_ref[...]
    lane = lax.broadcasted_iota(jnp.int32, (1, LANES), 1)
    first_half = (lane % HEAD_DIM) < (HEAD_DIM // 2)

    def rope_store(t, out_ref, scale):
        for j in range(d // LANES):
            tj = t[:, j * LANES:(j + 1) * LANES]
            partner = jnp.where(first_half, pltpu.roll(tj, LANES - HEAD_DIM // 2, 1),
                                pltpu.roll(tj, HEAD_DIM // 2, 1))
            r = tj * cosf + partner * sinf
            if scale != 1.0:
                r = r * scale
            out_ref[:, j * LANES:(j + 1) * LANES] = r.astype(BF16)

    rope_store(jnp.dot(xb, w_ref[:, 0:d], preferred_element_type=F32), q_ref, QK_SCALE)
    rope_store(jnp.dot(xb, w_ref[:, d:2 * d], preferred_element_type=F32), k_ref, 1.0)
    v_ref[...] = jnp.dot(xb, w_ref[:, 2 * d:3 * d], preferred_element_type=F32).astype(BF16)


def _diff_proj(x2, w_qkv, cosf, sinf, seq_len):
    n, d = x2.shape
    tm = ROW_TILE
    tiles_per_seq = seq_len // tm
    row_spec = pl.BlockSpec((tm, d), lambda i: (i, 0))
    rope_spec = pl.BlockSpec((tm, LANES), lambda i: (i % tiles_per_seq, 0))
    return pl.pallas_call(
        _diff_proj_kernel,
        grid=(n // tm,),
        in_specs=[row_spec, _resident(w_qkv.shape), rope_spec, rope_spec],
        out_specs=[row_spec, row_spec, row_spec],
        out_shape=[jax.ShapeDtypeStruct((n, d), BF16)] * 3,
        compiler_params=pltpu.CompilerParams(
            dimension_semantics=("parallel",), vmem_limit_bytes=VMEM_LIMIT_BYTES),
        name="diff_proj",
    )(x2, w_qkv, cosf, sinf)


def _diff_attn_kernel(q1_ref, q2_ref, k1_ref, k2_ref, v_ref, lam_ref, g_ref, o_ref,
                      *, tq, tk, lambda_init):
    half = pl.program_id(1) % 2
    seq_len = q1_ref.shape[1]
    lane = lax.broadcasted_iota(jnp.int32, (1, LANES), 1)
    mine = (lane // HEAD_DIM) == half
    chunk_visible = ((lax.broadcasted_iota(jnp.int32, (tq, tk), 1) // CHUNK)
                     - (lax.broadcasted_iota(jnp.int32, (tq, tk), 0) // CHUNK))
    zero_bf = jnp.zeros((), BF16)

    lp = lam_ref[...]
    lam = (jnp.exp(jnp.sum(lp[0:1, :] * lp[1:2, :], axis=1, keepdims=True))
           - jnp.exp(jnp.sum(lp[2:3, :] * lp[3:4, :], axis=1, keepdims=True)) + lambda_init)

    def q_block(qi, _):
        q_start = pl.multiple_of(qi * tq, tq)
        qs = (jnp.where(mine, q1_ref[0, pl.ds(q_start, tq), :], zero_bf),
              jnp.where(mine, q2_ref[0, pl.ds(q_start, tq), :], zero_bf))

        def kv_block(kj, carry, masked):
            k_start = pl.multiple_of(kj * tk, tk)
            ks = (k1_ref[0, pl.ds(k_start, tk), :], k2_ref[0, pl.ds(k_start, tk), :])
            v = v_ref[0, pl.ds(k_start, tk), :]
            out = []
            for t in range(2):
                m_prev, l_prev, acc = carry[t]
                s = lax.dot_general(qs[t], ks[t], NT_DIMS, preferred_element_type=F32)
                if masked:
                    s = jnp.where(chunk_visible <= (q_start - k_start) // CHUNK, s, -jnp.inf)
                m_new = jnp.maximum(m_prev, jnp.max(s, axis=1, keepdims=True))
                alpha = jnp.exp(m_prev - m_new)
                p = jnp.exp(s - m_new)
                l_new = alpha * l_prev + jnp.sum(p, axis=1, keepdims=True)
                acc = acc * alpha + jnp.dot(p.astype(BF16), v, preferred_element_type=F32)
                out.append((m_new, l_new, acc))
            return tuple(out)

        init = (jnp.full((tq, 1), -jnp.inf, F32), jnp.zeros((tq, 1), F32),
                jnp.zeros((tq, DIFF_V_DIM), F32))
        carry = (init, init)
        carry = lax.fori_loop(0, qi * (tq // tk), lambda j, c: kv_block(j, c, False), carry)
        for dj in range(tq // tk):
            carry = kv_block(qi * (tq // tk) + dj, carry, True)
        (_, l1, acc1), (_, l2, acc2) = carry
        o = acc1 * (1.0 / l1) - acc2 * (lam * (1.0 / l2))
        o = o * lax.rsqrt(jnp.mean(o * o, axis=1, keepdims=True) + RMS_EPS)
        o = o * g_ref[...] * (1.0 - lambda_init)
        o_ref[0, pl.ds(q_start, tq), :] = o.astype(BF16)
        return 0

    lax.fori_loop(0, seq_len // tq, q_block, 0)


def _diff_attn(q, k, v, lam_p, subln_g, lambda_init):
    b, s, d = q.shape
    pairs = DIFF_HEADS // 2
    tq, tk = ATTN_TQ, ATTN_TK

    def seq_spec(index):
        return pl.BlockSpec((1, s, LANES), lambda bi, hi: (bi, 0, index(hi)))

    return pl.pallas_call(
        functools.partial(_diff_attn_kernel, tq=tq, tk=tk, lambda_init=lambda_init),
        grid=(b, DIFF_HEADS),
        in_specs=[seq_spec(lambda h: h // 2), seq_spec(lambda h: pairs + h // 2),
                  seq_spec(lambda h: h // 2), seq_spec(lambda h: pairs + h // 2),
                  seq_spec(lambda h: h), _resident(lam_p.shape), _resident(subln_g.shape)],
        out_specs=seq_spec(lambda h: h),
        out_shape=jax.ShapeDtypeStruct((b, s, d), BF16),
        compiler_params=pltpu.CompilerParams(
            dimension_semantics=("parallel", "parallel"), vmem_limit_bytes=VMEM_LIMIT_BYTES),
        name="diff_attn",
    )(q, q, k, k, v, lam_p, subln_g)


def _rope_tables(seq_len):
    inv_freq = 1.0 / (ROPE_THETA ** (jnp.arange(0, HEAD_DIM, 2, dtype=F32) / HEAD_DIM))
    ang = jnp.arange(seq_len, dtype=F32)[:, None] * inv_freq[None, :]
    cos, sin = jnp.cos(ang), jnp.sin(ang)
    reps = LANES // (HEAD_DIM // 2)
    return jnp.tile(cos, (1, reps)), jnp.tile(jnp.concatenate([-sin, sin], axis=1), (1, reps // 2))


def kernel(x, a_w_in, a_b_f, a_w_out, kv_w, b_w_q, b_lambda, b_subln_g, b_w_out,
           ffn_w_in, ffn_conv_w, ffn_conv_b, ffn_w_out,
           ln_attn_g, ln_attn_b, ln_ffn_g, ln_ffn_b):
    b, s, d = x.shape
    n = b * s
    assert d == D_MODEL and s % ROW_TILE == 0 and s % ATTN_TQ == 0 and ATTN_TQ % ATTN_TK == 0
    x2 = x.reshape(n, d)
    row = lambda p: p.reshape(1, -1)

    w_in = a_w_in[0]
    q, k, v, c = _fox_proj(x2, w_in[:, :3 * d].astype(BF16), w_in[:, 3 * d:].astype(BF16),
                           row(a_b_f[0]), s)
    ccol = c.reshape(b, s, FOX_HEADS)
    crow = (ccol.transpose(0, 2, 1)
            .reshape(b, FOX_HEADS // 2, 2, s // ATTN_TK, ATTN_TK).transpose(0, 1, 3, 2, 4))
    o = _fox_attn(q.reshape(b, s, d), k.reshape(b, s, d), v.reshape(b, s, d), ccol, crow)
    x2 = _outproj_ln(o.reshape(n, d), a_w_out[0].astype(BF16), x2,
                     row(ln_attn_g[0]), row(ln_attn_b[0]))
    x2 = _ffn(x2, ffn_w_in[0], ffn_conv_w[0], ffn_conv_b[0], ffn_w_out[0],
              row(ln_ffn_g[0]), row(ln_ffn_b[0]), s)

    lambda_init = 0.8 - 0.6 * math.exp(-0.3 * 1)
    cosf, sinf = _rope_tables(s)
    w_qkv = jnp.concatenate([b_w_q[0], kv_w], axis=1).astype(BF16)
    q, k, v = _diff_proj(x2, w_qkv, cosf, sinf, s)
    o = _diff_attn(q.reshape(b, s, d), k.reshape(b, s, d), v.reshape(b, s, d),
                   b_lambda[0], row(b_subln_g[0]), lambda_init)
    x2 = _outproj_ln(o.reshape(n, d), b_w_out[0].astype(BF16), x2,
                     row(ln_attn_g[1]), row(ln_attn_b[1]))
    x2 = _ffn(x2, ffn_w_in[1], ffn_conv_w[1], ffn_conv_b[1], ffn_w_out[1],
              row(ln_ffn_g[1]), row(ln_ffn_b[1]), s)
    return x2.reshape(b, s, d)
```

```python
import functools
import math

import jax
import jax.numpy as jnp
from jax import lax
from jax.experimental import pallas as pl
from jax.experimental.pallas import tpu as pltpu

D_MODEL = 1024
HEAD_DIM = 64
FOX_HEADS = D_MODEL // HEAD_DIM
DIFF_HEADS = D_MODEL // (2 * HEAD_DIM)
DIFF_V_DIM = 2 * HEAD_DIM
D_FF = 2816
CHUNK = 64
DEPTH = 2
ROPE_THETA = 10000.0
LN_EPS = 1e-5
RMS_EPS = 1e-5
DN_ALPHA = (2 * DEPTH) ** 0.25
QK_SCALE = HEAD_DIM ** -0.5

LANES = 128
SUBLANES = 8
VMEM_LIMIT_BYTES = 56 * 1024 * 1024

ROW_TILE = 512
ATTN_TQ = 512
ATTN_TK = 256
FF_TILE = 256

F32 = jnp.float32
BF16 = jnp.bfloat16
NT_DIMS = (((1,), (1,)), ((), ()))


def _resident(shape):
    zeros = (0,) * len(shape)
    return pl.BlockSpec(shape, lambda *_: zeros, pipeline_mode=pl.Buffered(1))


def _layer_norm(y, g, b):
    mu = jnp.mean(y, axis=-1, keepdims=True)
    yc = y - mu
    var = jnp.mean(yc * yc, axis=-1, keepdims=True)
    return yc * lax.rsqrt(var + LN_EPS) * g + b


def _vt_out_spec(seq_len, d):
    tiles_per_seq = seq_len // ROW_TILE
    return pl.BlockSpec((1, ROW_TILE // ATTN_TK, d, ATTN_TK),
                        lambda i: (i // tiles_per_seq, i % tiles_per_seq, 0, 0))


def _store_vt(vt_ref, w_vt, xb):
    vt = lax.dot_general(w_vt, xb, NT_DIMS, preferred_element_type=F32).astype(BF16)
    for j in range(ROW_TILE // ATTN_TK):
        vt_ref[0, j] = vt[:, j * ATTN_TK:(j + 1) * ATTN_TK]


def _fox_proj_kernel(x_ref, w_ref, wvt_ref, wf_ref, bf_ref, q_ref, k_ref, vt_ref, c_ref,
                     carry_ref, *, tiles_per_seq):
    d = D_MODEL
    xb = x_ref[...].astype(BF16)
    q = jnp.dot(xb, w_ref[:, 0:d], preferred_element_type=F32)
    q_ref[...] = (q * QK_SCALE).astype(BF16)
    k_ref[...] = jnp.dot(xb, w_ref[:, d:2 * d], preferred_element_type=F32).astype(BF16)
    _store_vt(vt_ref, wvt_ref[...], xb)

    fl = jnp.dot(xb, wf_ref[...], preferred_element_type=F32) + bf_ref[...]
    g = jnp.minimum(fl, 0.0) - jnp.log1p(jnp.exp(-jnp.abs(fl)))

    tm = g.shape[0]
    row = lax.broadcasted_iota(jnp.int32, (tm, tm), 0)
    col = lax.broadcasted_iota(jnp.int32, (tm, tm), 1)
    tri = (col <= row).astype(BF16)
    g1 = g.astype(BF16)
    r1 = g - g1.astype(F32)
    g2 = r1.astype(BF16)
    g3 = (r1 - g2.astype(F32)).astype(BF16)
    cs = (jnp.dot(tri, g1, preferred_element_type=F32)
          + jnp.dot(tri, g2, preferred_element_type=F32)
          + jnp.dot(tri, g3, preferred_element_type=F32))

    @pl.when(pl.program_id(0) % tiles_per_seq == 0)
    def _():
        carry_ref[...] = jnp.zeros_like(carry_ref)

    c = cs + carry_ref[...]
    c_ref[...] = c
    carry_ref[...] = c[tm - 1:tm, :]


def _fox_proj(x2, w_qk, w_vt, w_f, b_f, batch, seq_len):
    n, d = x2.shape
    tm = ROW_TILE
    row_spec = pl.BlockSpec((tm, d), lambda i: (i, 0))
    return pl.pallas_call(
        functools.partial(_fox_proj_kernel, tiles_per_seq=seq_len // tm),
        grid=(n // tm,),
        in_specs=[row_spec, _resident(w_qk.shape), _resident(w_vt.shape), _resident(w_f.shape),
                  _resident(b_f.shape)],
        out_specs=[row_spec, row_spec, _vt_out_spec(seq_len, d),
                   pl.BlockSpec((tm, FOX_HEADS), lambda i: (i, 0))],
        out_shape=[jax.ShapeDtypeStruct((n, d), BF16), jax.ShapeDtypeStruct((n, d), BF16),
                   jax.ShapeDtypeStruct((batch, seq_len // ATTN_TK, d, ATTN_TK), BF16),
                   jax.ShapeDtypeStruct((n, FOX_HEADS), F32)],
        scratch_shapes=[pltpu.VMEM((1, FOX_HEADS), F32)],
        compiler_params=pltpu.CompilerParams(
            dimension_semantics=("arbitrary",), vmem_limit_bytes=VMEM_LIMIT_BYTES),
        name="fox_proj",
    )(x2, w_qk, w_vt, w_f, b_f)


def _fox_attn_kernel(q_ref, k_ref, vt_ref, ccol_ref, crow_ref, o_ref,
                     za_ref, zb_ref, mza_ref, mzb_ref, acc_ref, *, tq, tk):
    assert tq == 2 * tk
    seq_len = q_ref.shape[1]
    lane = lax.broadcasted_iota(jnp.int32, (1, LANES), 1)
    lo = lane < HEAD_DIM
    top = lax.broadcasted_iota(jnp.int32, (LANES, 1), 0) < HEAD_DIM
    query_minus_key = (lax.broadcasted_iota(jnp.int32, (tk, tq), 1)
                       - lax.broadcasted_iota(jnp.int32, (tk, tq), 0))
    zero_bf = jnp.zeros((), BF16)

    def q_block(qi, _):
        q_start = pl.multiple_of(qi * tq, tq)
        q = q_ref[0, pl.ds(q_start, tq), :]
        q_heads = (jnp.where(lo, q, zero_bf), jnp.where(lo, zero_bf, q))
        cb = crow_ref[0, 0, qi]

        def stage_scores(kb, z_ref, mz_ref, diag_offset=None):
            k_start = pl.multiple_of(kb * tk, tk)
            k = k_ref[0, pl.ds(k_start, tk), :]
            ck = ccol_ref[0, 0, pl.ds(k_start, tk), :]
            for h in range(2):
                s = lax.dot_general(k, q_heads[h], NT_DIMS, preferred_element_type=F32)
                z = s - ck[:, h:h + 1]
                if diag_offset is not None:
                    z = jnp.where(query_minus_key >= diag_offset, z, -jnp.inf)
                z_ref[h] = z
                mz_ref[h:h + 1, :] = jnp.max(z, axis=0, keepdims=True)

        def accumulate(kb, z_ref, mz_ref, stats):
            vt = vt_ref[0, kb]
            vt_heads = (jnp.where(top, vt, zero_bf), jnp.where(top, zero_bf, vt))
            new_stats, alphas, pv = [], [], None
            for h in range(2):
                m_prev, l_prev = stats[h]
                cbh = cb[h:h + 1, :]
                m_new = jnp.maximum(m_prev, mz_ref[h:h + 1, :] + cbh)
                alpha = jnp.exp(m_prev - m_new)
                p = jnp.exp(z_ref[h] - (m_new - cbh))
                l_new = alpha * l_prev + jnp.sum(p, axis=0, keepdims=True)
                part = jnp.dot(vt_heads[h], p.astype(BF16), preferred_element_type=F32)
                pv = part if pv is None else pv + part
                new_stats.append((m_new, l_new))
                alphas.append(alpha)
            acc_ref[0:HEAD_DIM, :] = acc_ref[0:HEAD_DIM, :] * alphas[0] + pv[0:HEAD_DIM, :]
            acc_ref[HEAD_DIM:, :] = acc_ref[HEAD_DIM:, :] * alphas[1] + pv[HEAD_DIM:, :]
            return tuple(new_stats)

        diag = qi * (tq // tk)
        acc_ref[...] = jnp.zeros_like(acc_ref)
        init_stat = (jnp.full((1, tq), -jnp.inf, F32), jnp.zeros((1, tq), F32))
        stage_scores(diag, za_ref, mza_ref, diag_offset=0)
        stage_scores(diag + 1, zb_ref, mzb_ref, diag_offset=tk)
        stats = accumulate(diag, za_ref, mza_ref, (init_stat, init_stat))

        def step(jj, stats):
            stage_scores(2 * jj, za_ref, mza_ref)
            stats = accumulate(jnp.where(jj == 0, diag + 1, 2 * jj - 1), zb_ref, mzb_ref, stats)
            stage_scores(2 * jj + 1, zb_ref, mzb_ref)
            return accumulate(2 * jj, za_ref, mza_ref, stats)

        stats = lax.fori_loop(0, qi, step, stats)
        stats = accumulate(jnp.where(qi == 0, diag + 1, 2 * qi - 1), zb_ref, mzb_ref, stats)
        out_t = jnp.concatenate([acc_ref[0:HEAD_DIM, :] * (1.0 / stats[0][1]),
                                 acc_ref[HEAD_DIM:, :] * (1.0 / stats[1][1])], axis=0)
        o_ref[0, pl.ds(q_start, tq), :] = out_t.T.astype(BF16)
        return 0

    lax.fori_loop(0, seq_len // tq, q_block, 0)


def _fox_attn(q, k, vt, ccol, crow):
    b, s, d = q.shape
    pairs = d // LANES
    tq, tk = ATTN_TQ, ATTN_TK
    seq_spec = pl.BlockSpec((1, s, LANES), lambda bi, pi: (bi, 0, pi))
    return pl.pallas_call(
        functools.partial(_fox_attn_kernel, tq=tq, tk=tk),
        grid=(b, pairs),
        in_specs=[seq_spec, seq_spec,
                  pl.BlockSpec((1, s // tk, LANES, tk), lambda bi, pi: (bi, 0, pi, 0)),
                  pl.BlockSpec((1, 1, s, 2), lambda bi, pi: (bi, pi, 0, 0)),
                  pl.BlockSpec((1, 1, s // tq, 2, tq), lambda bi, pi: (bi, pi, 0, 0, 0))],
        out_specs=seq_spec,
        out_shape=jax.ShapeDtypeStruct((b, s, d), BF16),
        scratch_shapes=[pltpu.VMEM((2, tk, tq), F32), pltpu.VMEM((2, tk, tq), F32),
                        pltpu.VMEM((2, tq), F32), pltpu.VMEM((2, tq), F32),
                        pltpu.VMEM((LANES, tq), F32)],
        compiler_params=pltpu.CompilerParams(
            dimension_semantics=("parallel", "parallel"), vmem_limit_bytes=VMEM_LIMIT_BYTES),
        name="fox_attn",
    )(q, k, vt, ccol, crow)


def _outproj_ln_kernel(o_ref, w_ref, x_ref, g_ref, b_ref, y_ref):
    h = jnp.dot(o_ref[...], w_ref[...], preferred_element_type=F32)
    y_ref[...] = _layer_norm(DN_ALPHA * x_ref[...] + h, g_ref[...], b_ref[...])


def _outproj_ln(o2, w, x2, g, b):
    n, d = x2.shape
    tm = ROW_TILE
    row_spec = pl.BlockSpec((tm, d), lambda i: (i, 0))
    return pl.pallas_call(
        _outproj_ln_kernel,
        grid=(n // tm,),
        in_specs=[row_spec, _resident(w.shape), row_spec, _resident(g.shape), _resident(b.shape)],
        out_specs=row_spec,
        out_shape=jax.ShapeDtypeStruct((n, d), F32),
        compiler_params=pltpu.CompilerParams(
            dimension_semantics=("parallel",), vmem_limit_bytes=VMEM_LIMIT_BYTES),
        name="outproj_ln",
    )(o2, w, x2, g, b)


def _ffn_kernel(x_ref, win_ref, cw_ref, cb_ref, wout_ref, g_ref, b_ref, y_ref,
                carry_ref, acc_ref, *, tiles_per_seq, n_chunks):
    @pl.when(pl.program_id(0) % tiles_per_seq == 0)
    def _():
        carry_ref[...] = jnp.zeros_like(carry_ref)

    x = x_ref[...]
    xb = x.astype(BF16)
    tm = x.shape[0]
    tf = win_ref.shape[2]
    row8 = lax.broadcasted_iota(jnp.int32, (SUBLANES, tf), 0)
    acc_ref[...] = jnp.zeros_like(acc_ref)

    def conv_half(idx):
        u = jnp.dot(xb, win_ref[idx], preferred_element_type=F32)
        prev = carry_ref[idx]
        carry_ref[idx] = u[tm - SUBLANES:tm, :]
        w = cw_ref[idx]
        out = w[2:3, :] * u + cb_ref[idx]
        for shift in (1, 2):
            rolled = pltpu.roll(u, shift, 0)
            top = jnp.where(row8 < shift, pltpu.roll(prev, shift, 0), rolled[0:SUBLANES, :])
            shifted = jnp.concatenate([top, rolled[SUBLANES:, :]], axis=0)
            out = out + w[2 - shift:3 - shift, :] * shifted
        return out

    def chunk(c, _):
        yg = conv_half(c)
        yv = conv_half(n_chunks + c)
        h = yg * (1.0 / (1.0 + jnp.exp(-yg))) * yv
        acc_ref[...] += jnp.dot(h.astype(BF16), wout_ref[c], preferred_element_type=F32)
        return 0

    lax.fori_loop(0, n_chunks, chunk, 0)
    y_ref[...] = _layer_norm(DN_ALPHA * x + acc_ref[...], g_ref[...], b_ref[...])


def _ffn(x2, w_in, conv_w, conv_b, w_out, g, b, seq_len):
    n, d = x2.shape
    tm, tf = ROW_TILE, FF_TILE
    n_chunks = D_FF // tf
    win3 = w_in.astype(BF16).reshape(d, 2 * n_chunks, tf).transpose(1, 0, 2)
    cw3 = conv_w.reshape(3, 2 * n_chunks, tf).transpose(1, 0, 2)
    cb3 = conv_b.reshape(2 * n_chunks, 1, tf)
    wout3 = w_out.astype(BF16).reshape(n_chunks, tf, d)
    row_spec = pl.BlockSpec((tm, d), lambda i: (i, 0))
    return pl.pallas_call(
        functools.partial(_ffn_kernel, tiles_per_seq=seq_len // tm, n_chunks=n_chunks),
        grid=(n // tm,),
        in_specs=[row_spec, _resident(win3.shape), _resident(cw3.shape), _resident(cb3.shape),
                  _resident(wout3.shape), _resident(g.shape), _resident(b.shape)],
        out_specs=row_spec,
        out_shape=jax.ShapeDtypeStruct((n, d), F32),
        scratch_shapes=[pltpu.VMEM((2 * n_chunks, SUBLANES, tf), F32),
                        pltpu.VMEM((tm, d), F32)],
        compiler_params=pltpu.CompilerParams(
            dimension_semantics=("arbitrary",), vmem_limit_bytes=VMEM_LIMIT_BYTES),
        name="conv_ffn",
    )(x2, win3, cw3, cb3, wout3, g, b)


def _diff_proj_kernel(x_ref, w_ref, wvt_ref, cos_ref, sin_ref, q_ref, k_ref, vt_ref):
    d = D_MODEL
    xb = x_ref[...].astype(BF16)
    cosf = cos_ref[...]
    sinf = sin_ref[...]
    lane = lax.broadcasted_iota(jnp.int32, (1, LANES), 1)
    first_half = (lane % HEAD_DIM) < (HEAD_DIM // 2)

    def rope_store(t, out_ref, scale):
        for j in range(d // LANES):
            tj = t[:, j * LANES:(j + 1) * LANES]
            partner = jnp.where(first_half, pltpu.roll(tj, LANES - HEAD_DIM // 2, 1),
                                pltpu.roll(tj, HEAD_DIM // 2, 1))
            r = tj * cosf + partner * sinf
            if scale != 1.0:
                r = r * scale
            out_ref[:, j * LANES:(j + 1) * LANES] = r.astype(BF16)

    rope_store(jnp.dot(xb, w_ref[:, 0:d], preferred_element_type=F32), q_ref, QK_SCALE)
    rope_store(jnp.dot(xb, w_ref[:, d:2 * d], preferred_element_type=F32), k_ref, 1.0)
    _store_vt(vt_ref, wvt_ref[...], xb)


def _diff_proj(x2, w_qk, w_vt, cosf, sinf, batch, seq_len):
    n, d = x2.shape
    tm = ROW_TILE
    tiles_per_seq = seq_len // tm
    row_spec = pl.BlockSpec((tm, d), lambda i: (i, 0))
    rope_spec = pl.BlockSpec((tm, LANES), lambda i: (i % tiles_per_seq, 0))
    return pl.pallas_call(
        _diff_proj_kernel,
        grid=(n // tm,),
        in_specs=[row_spec, _resident(w_qk.shape), _resident(w_vt.shape), rope_spec, rope_spec],
        out_specs=[row_spec, row_spec, _vt_out_spec(seq_len, d)],
        out_shape=[jax.ShapeDtypeStruct((n, d), BF16), jax.ShapeDtypeStruct((n, d), BF16),
                   jax.ShapeDtypeStruct((batch, seq_len // ATTN_TK, d, ATTN_TK), BF16)],
        compiler_params=pltpu.CompilerParams(
            dimension_semantics=("parallel",), vmem_limit_bytes=VMEM_LIMIT_BYTES),
        name="diff_proj",
    )(x2, w_qk, w_vt, cosf, sinf)


def _diff_attn_kernel(q1_ref, q2_ref, k1_ref, k2_ref, vt_ref, lam_ref, g_ref, o_ref,
                      za_ref, zb_ref, mza_ref, mzb_ref, acc_ref, *, tq, tk, lambda_init):
    assert tq == 2 * tk
    half = pl.program_id(1) % 2
    seq_len = q1_ref.shape[1]
    lane = lax.broadcasted_iota(jnp.int32, (1, LANES), 1)
    mine = (lane // HEAD_DIM) == half
    query_minus_key_chunk = ((lax.broadcasted_iota(jnp.int32, (tk, tq), 1) // CHUNK)
                             - (lax.broadcasted_iota(jnp.int32, (tk, tq), 0) // CHUNK))
    zero_bf = jnp.zeros((), BF16)

    lp = lam_ref[...]
    lam = (jnp.exp(jnp.sum(lp[0:1, :] * lp[1:2, :], axis=1, keepdims=True))
           - jnp.exp(jnp.sum(lp[2:3, :] * lp[3:4, :], axis=1, keepdims=True)) + lambda_init)

    def q_block(qi, _):
        q_start = pl.multiple_of(qi * tq, tq)
        qs = (jnp.where(mine, q1_ref[0, pl.ds(q_start, tq), :], zero_bf),
              jnp.where(mine, q2_ref[0, pl.ds(q_start, tq), :], zero_bf))
        k_refs = (k1_ref, k2_ref)

        def stage_scores(kb, z_ref, mz_ref, diag_chunks=None):
            k_start = pl.multiple_of(kb * tk, tk)
            for t in range(2):
                k = k_refs[t][0, pl.ds(k_start, tk), :]
                s = lax.dot_general(k, qs[t], NT_DIMS, preferred_element_type=F32)
                if diag_chunks is not None:
                    s = jnp.where(query_minus_key_chunk >= diag_chunks, s, -jnp.inf)
                z_ref[t] = s
                mz_ref[t:t + 1, :] = jnp.max(s, axis=0, keepdims=True)

        def accumulate(kb, z_ref, mz_ref, stats):
            vt = vt_ref[0, kb]
            new_stats = []
            for t in range(2):
                m_prev, l_prev = stats[t]
                m_new = jnp.maximum(m_prev, mz_ref[t:t + 1, :])
                alpha = jnp.exp(m_prev - m_new)
                p = jnp.exp(z_ref[t] - m_new)
                l_new = alpha * l_prev + jnp.sum(p, axis=0, keepdims=True)
                acc_ref[t] = acc_ref[t] * alpha + jnp.dot(vt, p.astype(BF16),
                                                          preferred_element_type=F32)
                new_stats.append((m_new, l_new))
            return tuple(new_stats)

        diag = qi * (tq // tk)
        acc_ref[...] = jnp.zeros_like(acc_ref)
        init_stat = (jnp.full((1, tq), -jnp.inf, F32), jnp.zeros((1, tq), F32))
        stage_scores(diag, za_ref, mza_ref, diag_chunks=0)
        stage_scores(diag + 1, zb_ref, mzb_ref, diag_chunks=tk // CHUNK)
        stats = accumulate(diag, za_ref, mza_ref, (init_stat, init_stat))

        def step(jj, stats):
            stage_scores(2 * jj, za_ref, mza_ref)
            stats = accumulate(jnp.where(jj == 0, diag + 1, 2 * jj - 1), zb_ref, mzb_ref, stats)
            stage_scores(2 * jj + 1, zb_ref, mzb_ref)
            return accumulate(2 * jj, za_ref, mza_ref, stats)

        stats = lax.fori_loop(0, qi, step, stats)
        (_, l1), (_, l2) = accumulate(jnp.where(qi == 0, diag + 1, 2 * qi - 1),
                                      zb_ref, mzb_ref, stats)
        o = acc_ref[0] * (1.0 / l1) - acc_ref[1] * (lam * (1.0 / l2))
        o = o * lax.rsqrt(jnp.mean(o * o, axis=0, keepdims=True) + RMS_EPS)
        o = o * g_ref[...] * (1.0 - lambda_init)
        o_ref[0, pl.ds(q_start, tq), :] = o.T.astype(BF16)
        return 0

    lax.fori_loop(0, seq_len // tq, q_block, 0)


def _diff_attn(q, k, vt, lam_p, subln_g, lambda_init):
    b, s, d = q.shape
    pairs = DIFF_HEADS // 2
    tq, tk = ATTN_TQ, ATTN_TK

    def seq_spec(index):
        return pl.BlockSpec((1, s, LANES), lambda bi, hi: (bi, 0, index(hi)))

    return pl.pallas_call(
        functools.partial(_diff_attn_kernel, tq=tq, tk=tk, lambda_init=lambda_init),
        grid=(b, DIFF_HEADS),
        in_specs=[seq_spec(lambda h: h // 2), seq_spec(lambda h: pairs + h // 2),
                  seq_spec(lambda h: h // 2), seq_spec(lambda h: pairs + h // 2),
                  pl.BlockSpec((1, s // tk, DIFF_V_DIM, tk), lambda bi, hi: (bi, 0, hi, 0)),
                  _resident(lam_p.shape), _resident(subln_g.shape)],
        out_specs=seq_spec(lambda h: h),
        out_shape=jax.ShapeDtypeStruct((b, s, d), BF16),
        scratch_shapes=[pltpu.VMEM((2, tk, tq), F32), pltpu.VMEM((2, tk, tq), F32),
                        pltpu.VMEM((2, tq), F32), pltpu.VMEM((2, tq), F32),
                        pltpu.VMEM((2, DIFF_V_DIM, tq), F32)],
        compiler_params=pltpu.CompilerParams(
            dimension_semantics=("parallel", "parallel"), vmem_limit_bytes=VMEM_LIMIT_BYTES),
        name="diff_attn",
    )(q, q, k, k, vt, lam_p, subln_g)


def _rope_tables(seq_len):
    inv_freq = 1.0 / (ROPE_THETA ** (jnp.arange(0, HEAD_DIM, 2, dtype=F32) / HEAD_DIM))
    ang = jnp.arange(seq_len, dtype=F32)[:, None] * inv_freq[None, :]
    cos, sin = jnp.cos(ang), jnp.sin(ang)
    reps = LANES // (HEAD_DIM // 2)
    return jnp.tile(cos, (1, reps)), jnp.tile(jnp.concatenate([-sin, sin], axis=1), (1, reps // 2))


def kernel(x, a_w_in, a_b_f, a_w_out, kv_w, b_w_q, b_lambda, b_subln_g, b_w_out,
           ffn_w_in, ffn_conv_w, ffn_conv_b, ffn_w_out,
           ln_attn_g, ln_attn_b, ln_ffn_g, ln_ffn_b):
    b, s, d = x.shape
    n = b * s
    assert d == D_MODEL and s % ROW_TILE == 0 and s % ATTN_TQ == 0 and ATTN_TQ % ATTN_TK == 0
    x2 = x.reshape(n, d)
    row = lambda p: p.reshape(1, -1)
    pairs = FOX_HEADS // 2

    w_in = a_w_in[0]
    q, k, vt, c = _fox_proj(x2, w_in[:, :2 * d].astype(BF16), w_in[:, 2 * d:3 * d].T.astype(BF16),
                            w_in[:, 3 * d:].astype(BF16), row(a_b_f[0]), b, s)
    c = c.reshape(b, s, pairs, 2)
    ccol = c.transpose(0, 2, 1, 3)
    crow = c.reshape(b, s // ATTN_TQ, ATTN_TQ, pairs, 2).transpose(0, 3, 1, 4, 2)
    o = _fox_attn(q.reshape(b, s, d), k.reshape(b, s, d), vt, ccol, crow)
    x2 = _outproj_ln(o.reshape(n, d), a_w_out[0].astype(BF16), x2,
                     row(ln_attn_g[0]), row(ln_attn_b[0]))
    x2 = _ffn(x2, ffn_w_in[0], ffn_conv_w[0], ffn_conv_b[0], ffn_w_out[0],
              row(ln_ffn_g[0]), row(ln_ffn_b[0]), s)

    lambda_init = 0.8 - 0.6 * math.exp(-0.3 * 1)
    cosf, sinf = _rope_tables(s)
    w_qk = jnp.concatenate([b_w_q[0], kv_w[:, :d]], axis=1).astype(BF16)
    q, k, vt = _diff_proj(x2, w_qk, kv_w[:, d:].T.astype(BF16), cosf, sinf, b, s)
    o = _diff_attn(q.reshape(b, s, d), k.reshape(b, s, d), vt,
                   b_lambda[0], b_subln_g[0].reshape(-1, 1), lambda_init)
    x2 = _outproj_ln(o.reshape(n, d), b_w_out[0].astype(BF16), x2,
                     row(ln_attn_g[1]), row(ln_attn_b[1]))
    x2 = _ffn(x2, ffn_w_in[1], ffn_conv_w[1], ffn_conv_b[1], ffn_w_out[1],
              row(ln_ffn_g[1]), row(ln_ffn_b[1]), s)
    return x2.reshape(b, s, d)
```
